```python
import math
import jax, jax.numpy as jnp
from jax import lax
import numpy as np

D_MODEL = 4096
BATCH = 4
SEQ = 2048
DEPTH = 2
DEC_BATCH = 8
DEC_SEQ = 8
PAST_LEN = 16384
PAGE_SIZE = 128

EPS = 1e-6
N_BRANCH = 4
MIX_W = D_MODEL // 4
GLA_HEADS = 4
GLA_DK = 128
GLA_DV = MIX_W // GLA_HEADS
GLA_RANK = 16
GLA_GATE_NORM = 16.0
GLA_CHUNK = 16
GDN_HEADS = 8
GDN_DK = 128
GDN_DV = MIX_W // GDN_HEADS
GDN_CONV = 4
GDN_CHUNK = 64
GDN_DT_MIN = 0.001
GDN_DT_MAX = 0.1
S5_GROUP = 16
S5_GROUPS = MIX_W // S5_GROUP
S5_STATE = 64
S5_DT_MIN = 0.001
S5_DT_MAX = 0.1
DSA_HEAD_DIM = 128
DSA_HEADS_PER_GROUP = MIX_W // DSA_HEAD_DIM
DSA_PATTERNS = ((128, 1), (512, 4), (2048, 16))
DSA_GROUPS = len(DSA_PATTERNS)
DSA_QBLOCK = 128
MEM_TOKENS = 256
MEM_HEADS = 4
MEM_HEAD_DIM = 128
MOE_GROUPS = 4
MOE_EXPERTS_PER_GROUP = 8
MOE_EXPERTS = MOE_GROUPS * MOE_EXPERTS_PER_GROUP
MOE_TOP_K = 2
MOE_FF = 1024
MOE_BLOCK = 128

IN_SPLITS = (
    GLA_HEADS * GLA_DK, GLA_HEADS * GLA_DK, GLA_HEADS * GLA_DV, GLA_RANK, GLA_HEADS * GLA_DV,
    GDN_HEADS * GDN_DK, GDN_HEADS * GDN_DK, GDN_HEADS * GDN_DV, GDN_HEADS * GDN_DV, GDN_HEADS, GDN_HEADS,
    MIX_W,
    DSA_GROUPS * MIX_W, DSA_GROUPS * MIX_W, DSA_GROUPS * MIX_W,
    N_BRANCH * D_MODEL,
)
IN_COLS = sum(IN_SPLITS)
SPLIT_AT = tuple(sum(IN_SPLITS[:i + 1]) for i in range(len(IN_SPLITS) - 1))

kernel_name = 'hybrid_gla_gdn_s5_dilated_hmoe_step'


def rms_norm(x, gain):
    xf = x.astype(jnp.float32)
    y = xf * lax.rsqrt(jnp.mean(xf * xf, axis=-1, keepdims=True) + EPS)
    return (y * gain.astype(jnp.float32)).astype(x.dtype)


def l2_norm(x):
    xf = x.astype(jnp.float32)
    return xf * lax.rsqrt(jnp.sum(xf * xf, axis=-1, keepdims=True) + EPS)


def to_chunks(x, c):
    b, t, h, f = x.shape
    return x.reshape(b, t // c, c, h, f).transpose(1, 0, 3, 2, 4)


def from_chunks(x):
    n, b, h, c, f = x.shape
    return x.transpose(1, 0, 3, 2, 4).reshape(b, n * c, h, f)


def gla_recurrence(q, k, v, glog, s0):
    t = q.shape[1]
    c = math.gcd(t, GLA_CHUNK)
    qc, kc, vc = to_chunks(q, c), to_chunks(k, c), to_chunks(v, c)
    bc = jnp.cumsum(to_chunks(glog, c), axis=3)
    incl = jnp.tril(jnp.ones((c, c), bool))

    def step(s, inp):
        qi, ki, vi, bi = inp
        diff = bi[:, :, :, None, :] - bi[:, :, None, :, :]
        dec = jnp.exp(jnp.where(incl[:, :, None], diff, -jnp.inf))
        att = jnp.einsum('bhtd,bhsd,bhtsd->bhts', qi, ki, dec)
        o = jnp.einsum('bhtd,bhdv->bhtv', qi * jnp.exp(bi), s) + jnp.einsum('bhts,bhsv->bhtv', att, vi)
        blast = bi[:, :, -1]
        s = jnp.exp(blast)[..., None] * s + jnp.einsum('bhsd,bhsv->bhdv', ki * jnp.exp(blast[:, :, None] - bi), vi)
        return s, o

    s, o = lax.scan(step, s0, (qc, kc, vc, bc))
    return from_chunks(o), s


def gla_branch(q, k, v, lr, r, w_a2, b_a2, g_out, s0):
    b, t, _ = q.shape
    q = q.astype(jnp.float32).reshape(b, t, GLA_HEADS, GLA_DK) * GLA_DK ** -0.5
    k = k.astype(jnp.float32).reshape(b, t, GLA_HEADS, GLA_DK)
    v = v.astype(jnp.float32).reshape(b, t, GLA_HEADS, GLA_DV)
    glog = jax.nn.log_sigmoid((lr @ w_a2 + b_a2).astype(jnp.float32)) / GLA_GATE_NORM
    glog = glog.reshape(b, t, GLA_HEADS, GLA_DK)
    o, s_new = gla_recurrence(q, k, v, glog, s0.astype(jnp.float32))
    o = rms_norm(o, g_out) * jax.nn.silu(r.astype(jnp.float32).reshape(b, t, GLA_HEADS, GLA_DV))
    return o.reshape(b, t, MIX_W), s_new


def gated_delta_recurrence(q, k, v, g, beta, s0):
    t = q.shape[1]
    dv = v.shape[-1]
    c = math.gcd(t, GDN_CHUNK)
    qc, kc, vc = to_chunks(q, c), to_chunks(k, c), to_chunks(v, c)
    gc = jnp.cumsum(to_chunks(g[..., None], c)[..., 0], axis=-1)
    bc = to_chunks(beta[..., None], c)[..., 0]
    diff = gc[..., :, None] - gc[..., None, :]
    strict = jnp.tril(jnp.ones((c, c), bool), -1)
    incl = jnp.tril(jnp.ones((c, c), bool))
    a = bc[..., None] * jnp.einsum('nbhid,nbhjd->nbhij', kc, kc) * jnp.exp(jnp.where(strict, diff, -jnp.inf))
    rhs = bc[..., None] * jnp.concatenate([vc, kc * jnp.exp(gc)[..., None]], axis=-1)
    sol = lax.linalg.triangular_solve(a, rhs, left_side=True, lower=True, unit_diagonal=True)
    u_c, w_c = sol[..., :dv], sol[..., dv:]
    qk = jnp.einsum('nbhid,nbhjd->nbhij', qc, kc) * jnp.exp(jnp.where(incl, diff, -jnp.inf))
    qg = qc * jnp.exp(gc)[..., None]
    glast = gc[..., -1]
    kdec = kc * jnp.exp(glast[..., None] - gc)[..., None]

    def step(s, inp):
        u_i, w_i, qg_i, qk_i, kd_i, gl_i = inp
        delta = u_i - jnp.einsum('bhcd,bhdv->bhcv', w_i, s)
        o = jnp.einsum('bhcd,bhdv->bhcv', qg_i, s) + jnp.einsum('bhij,bhjv->bhiv', qk_i, delta)
        s = jnp.exp(gl_i)[..., None, None] * s + jnp.einsum('bhcd,bhcv->bhdv', kd_i, delta)
        return s, o

    s, o = lax.scan(step, s0, (u_c, w_c, qg, qk, kdec, glast))
    return from_chunks(o), s


def gdn_branch(q, k, v, z, beta_in, a_in, conv_w, a_log, dt_bias, g_out, s0, conv_buf):
    b, t, _ = q.shape
    qkv = jnp.concatenate([q, k, v], axis=-1)
    xp = jnp.concatenate([conv_buf.astype(qkv.dtype), qkv], axis=1)
    conv = lax.conv_general_dilated(xp, conv_w.astype(qkv.dtype)[:, None, :], (1,), 'VALID',
                                    dimension_numbers=('NWC', 'WIO', 'NWC'),
                                    feature_group_count=xp.shape[-1])
    new_buf = xp[:, t:]
    q, k, v = jnp.split(jax.nn.silu(conv), 3, axis=-1)
    q = l2_norm(q.reshape(b, t, GDN_HEADS, GDN_DK)) * GDN_DK ** -0.5
    k = l2_norm(k.reshape(b, t, GDN_HEADS, GDN_DK))
    v = v.astype(jnp.float32).reshape(b, t, GDN_HEADS, GDN_DV)
    beta = jax.nn.sigmoid(beta_in.astype(jnp.float32))
    g = -jnp.exp(a_log.astype(jnp.float32)) * jax.nn.softplus(a_in.astype(jnp.float32) + dt_bias.astype(jnp.float32))
    o, s_new = gated_delta_recurrence(q, k, v, g, beta, s0.astype(jnp.float32))
    o = rms_norm(o, g_out) * jax.nn.silu(z.astype(jnp.float32).reshape(b, t, GDN_HEADS, GDN_DV))
    return o.reshape(b, t, MIX_W), s_new, new_buf


def s5_branch(u, a_re, a_im, log_dt, b_re, b_im, c_re, c_im, d_skip, w_glu, b_glu, h0):
    b, t, _ = u.shape
    uf = u.astype(jnp.float32).reshape(b, t, S5_GROUPS, S5_GROUP)
    dt = jnp.exp(log_dt.astype(jnp.float32))[:, None]
    ar, ai = a_re.astype(jnp.float32), a_im.astype(jnp.float32)
    mag = jnp.exp(ar * dt)
    abar_re, abar_im = mag * jnp.cos(ai * dt), mag * jnp.sin(ai * dt)
    den = ar * ar + ai * ai
    f_re = ((abar_re - 1.0) * ar + abar_im * ai) / den
    f_im = (abar_im * ar - (abar_re - 1.0) * ai) / den
    bbar_re = f_re[..., None] * b_re - f_im[..., None] * b_im
    bbar_im = f_re[..., None] * b_im + f_im[..., None] * b_re
    bu_re = jnp.einsum('gpi,btgi->btgp', bbar_re, uf)
    bu_im = jnp.einsum('gpi,btgi->btgp', bbar_im, uf)
    h0r, h0i = h0[..., 0].astype(jnp.float32), h0[..., 1].astype(jnp.float32)
    bu_re = bu_re.at[:, 0].add(abar_re * h0r - abar_im * h0i)
    bu_im = bu_im.at[:, 0].add(abar_re * h0i + abar_im * h0r)
    a_re_t = jnp.broadcast_to(abar_re, bu_re.shape)
    a_im_t = jnp.broadcast_to(abar_im, bu_re.shape)

    def combine(e1, e2):
        a1r, a1i, b1r, b1i = e1
        a2r, a2i, b2r, b2i = e2
        return (a2r * a1r - a2i * a1i, a2r * a1i + a2i * a1r,
                a2r * b1r - a2i * b1i + b2r, a2r * b1i + a2i * b1r + b2i)

    _, _, hr, hi = lax.associative_scan(combine, (a_re_t, a_im_t, bu_re, bu_im), axis=1)
    y = jnp.einsum('gip,btgp->btgi', c_re, hr) - jnp.einsum('gip,btgp->btgi', c_im, hi)
    y = jax.nn.gelu(y.reshape(b, t, MIX_W) + d_skip * u.astype(jnp.float32))
    out = y * jax.nn.sigmoid(y @ w_glu + b_glu)
    h_last = jnp.stack([hr[:, -1], hi[:, -1]], axis=-1)
    return out, h_last


def dilated_attn_prompt(q, k, v, window, dil):
    b, t, h, hd = q.shape
    sub = window // dil
    n = t // dil
    qb = math.gcd(n, DSA_QBLOCK)
    nb = n // qb
    kb_len = qb + sub

    def by_residue(x):
        return x.reshape(b, n, dil, h, hd).transpose(0, 2, 1, 3, 4)

    qr, kr, vr = by_residue(q), by_residue(k), by_residue(v)
    pad = ((0, 0), (0, 0), (sub, 0), (0, 0), (0, 0))
    kp, vp = jnp.pad(kr, pad), jnp.pad(vr, pad)
    blk0 = jnp.arange(nb) * qb
    idx = blk0[:, None] + jnp.arange(kb_len)[None, :]
    kg, vg = kp[:, :, idx], vp[:, :, idx]
    s = jnp.einsum('brcqhd,brckhd->brchqk', qr.reshape(b, dil, nb, qb, h, hd), kg) * hd ** -0.5
    rel = jnp.arange(qb)[:, None] + sub - jnp.arange(kb_len)[None, :]
    key_sub = blk0[:, None] - sub + jnp.arange(kb_len)[None, :]
    mask = ((rel >= 0) & (rel <= sub))[None] & (key_sub >= 0)[:, None, :]
    s = jnp.where(mask[None, None, :, None], s, -jnp.inf)
    lse = jax.nn.logsumexp(s, axis=-1)
    o = jnp.einsum('brchqk,brckhd->brcqhd', jnp.exp(s - lse[..., None]), vg)
    o = o.reshape(b, dil, n, h, hd).transpose(0, 2, 1, 3, 4).reshape(b, t, h, hd)
    lse = lse.transpose(0, 1, 2, 4, 3).reshape(b, dil, n, h).transpose(0, 2, 1, 3).reshape(b, t, h)
    return o, lse


def dilated_attn_sample(q, k, v, buf, window, dil):
    b, t, h, hd = q.shape
    sub = window // dil
    past = buf.shape[1]
    kc = jnp.concatenate([buf[:, :, 0].astype(jnp.float32), k], axis=1)
    vc = jnp.concatenate([buf[:, :, 1].astype(jnp.float32), v], axis=1)
    idx = past + jnp.arange(t)[:, None] - dil * jnp.arange(sub + 1)[None, :]
    valid = idx >= 0
    idx = jnp.maximum(idx, 0)
    kg, vg = kc[:, idx], vc[:, idx]
    s = jnp.einsum('bthd,btjhd->bhtj', q, kg) * hd ** -0.5
    s = jnp.where(valid[None, None], s, -jnp.inf)
    lse = jax.nn.logsumexp(s, axis=-1)
    o = jnp.einsum('bhtj,btjhd->bthd', jnp.exp(s - lse[..., None]), vg)
    new_buf = jnp.stack([kc, vc], axis=2)[:, t:]
    return o, lse.transpose(0, 2, 1), new_buf


def dsa_branch(q, k, v, q_norm, k_norm, win_bufs):
    b, t, _ = q.shape
    shp = (b, t, DSA_GROUPS, DSA_HEADS_PER_GROUP, DSA_HEAD_DIM)
    qh = rms_norm(q.reshape(shp), q_norm).astype(jnp.float32)
    kh = rms_norm(k.reshape(shp), k_norm).astype(jnp.float32)
    vh = v.reshape(shp).astype(jnp.float32)
    outs, lses, bufs = [], [], []
    for gi, (window, dil) in enumerate(DSA_PATTERNS):
        qg, kg, vg = qh[:, :, gi], kh[:, :, gi], vh[:, :, gi]
        if win_bufs is None:
            o, lse = dilated_attn_prompt(qg, kg, vg, window, dil)
            buf = jnp.stack([kg, vg], axis=2)[:, t - min(window, t):]
        else:
            o, lse, buf = dilated_attn_sample(qg, kg, vg, win_bufs[gi], window, dil)
        outs.append(o)
        lses.append(lse)
        bufs.append(buf)
    mix = jax.nn.softmax(jnp.stack(lses, axis=0), axis=0)[..., None]
    o = jnp.sum(mix * jnp.stack(outs, axis=0), axis=0)
    return o.reshape(b, t, MIX_W), bufs


def mem_kv(mem, g_src, w_kv, k_norm):
    b, m, _ = mem.shape
    kv = (rms_norm(mem, g_src) @ w_kv).reshape(b, m, 2, MEM_HEADS, MEM_HEAD_DIM)
    return jnp.stack([rms_norm(kv[:, :, 0], k_norm), kv[:, :, 1]], axis=2)


def mem_attend(xn, kv, w_q, q_norm, w_o):
    b, t, _ = xn.shape
    q = rms_norm((xn @ w_q).reshape(b, t, MEM_HEADS, MEM_HEAD_DIM), q_norm).astype(jnp.float32)
    k, v = kv[:, :, 0].astype(jnp.float32), kv[:, :, 1].astype(jnp.float32)
    s = jnp.einsum('bthd,bmhd->bhtm', q, k) * MEM_HEAD_DIM ** -0.5
    o = jnp.einsum('bhtm,bmhd->bthd', jax.nn.softmax(s, axis=-1), v)
    return o.reshape(b, t, MEM_HEADS * MEM_HEAD_DIM) @ w_o


def hier_moe(xn, w_coarse, b_coarse, w_fine, b_fine, w_gate, w_up, w_down):
    b, t, d = xn.shape
    n_tok = b * t
    x2 = xn.reshape(n_tok, d)
    logit_c = (x2 @ w_coarse + b_coarse).astype(jnp.float32)
    p_c = jax.nn.softmax(logit_c, axis=-1)
    grp = jnp.argmax(logit_c, axis=-1)
    p_grp = jnp.take_along_axis(p_c, grp[:, None], axis=1)[:, 0]
    logit_f = (x2 @ w_fine + b_fine).astype(jnp.float32).reshape(n_tok, MOE_GROUPS, MOE_EXPERTS_PER_GROUP)
    logit_f = jnp.take_along_axis(logit_f, grp[:, None, None], axis=1)[:, 0]
    top_val, top_idx = lax.top_k(logit_f, MOE_TOP_K)
    wts = p_grp[:, None] * jax.nn.softmax(top_val, axis=-1)
    expert = grp[:, None] * MOE_EXPERTS_PER_GROUP + top_idx
    m = n_tok * MOE_TOP_K
    e_flat = expert.reshape(m)
    tok_flat = jnp.repeat(jnp.arange(n_tok), MOE_TOP_K)
    w_flat = wts.reshape(m)
    order = jnp.argsort(e_flat)
    e_s, tok_s, w_s = e_flat[order], tok_flat[order], w_flat[order]
    counts = jax.ops.segment_sum(jnp.ones((m,), jnp.int32), e_flat, num_segments=MOE_EXPERTS)
    padded = (counts + MOE_BLOCK - 1) // MOE_BLOCK * MOE_BLOCK
    start = jnp.cumsum(counts) - counts
    pend = jnp.cumsum(padded)
    pstart = pend - padded
    dest = pstart[e_s] + jnp.arange(m) - start[e_s]
    n_blocks = m // MOE_BLOCK + MOE_EXPERTS
    xb = jnp.zeros((n_blocks * MOE_BLOCK, d), x2.dtype).at[dest].set(x2[tok_s])
    block_expert = jnp.minimum(
        jnp.sum(jnp.arange(n_blocks)[:, None] * MOE_BLOCK >= pend[None, :], axis=1), MOE_EXPERTS - 1)

    def run_block(args):
        xblk, e = args
        hid = jax.nn.silu(xblk @ w_gate[e]) * (xblk @ w_up[e])
        return hid @ w_down[e]

    yb = lax.map(run_block, (xb.reshape(n_blocks, MOE_BLOCK, d), block_expert))
    yb = yb.reshape(n_blocks * MOE_BLOCK, d)
    y = jnp.zeros((n_tok, d), jnp.float32).at[tok_s].add(yb[dest].astype(jnp.float32) * w_s[:, None])
    return y.reshape(b, t, d)


def decoder_layer(x, wl, gla_s0, gdn_s0, conv_buf, s5_s0, win_bufs, mem_kv_l):
    b, t, _ = x.shape
    xn = rms_norm(x, wl['norm_mix'])
    (a_q, a_k, a_v, a_lr, a_r, b_q, b_k, b_v, b_z, b_beta, b_a, c_u,
     d_q, d_k, d_v, gates) = jnp.split(xn @ wl['w_in'], SPLIT_AT, axis=-1)
    o_a, gla_s = gla_branch(a_q, a_k, a_v, a_lr, a_r, wl['gla_w_a2'], wl['gla_b_a2'], wl['gla_norm'], gla_s0)
    o_b, gdn_s, conv_new = gdn_branch(b_q, b_k, b_v, b_z, b_beta, b_a, wl['gdn_conv'], wl['gdn_a_log'],
                                      wl['gdn_dt_bias'], wl['gdn_norm'], gdn_s0, conv_buf)
    o_c, s5_s = s5_branch(c_u, wl['s5_a_re'], wl['s5_a_im'], wl['s5_log_dt'], wl['s5_b_re'], wl['s5_b_im'],
                          wl['s5_c_re'], wl['s5_c_im'], wl['s5_d'], wl['s5_w_glu'], wl['s5_b_glu'], s5_s0)
    o_d, win_new = dsa_branch(d_q, d_k, d_v, wl['dsa_q_norm'], wl['dsa_k_norm'], win_bufs)
    gate = jax.nn.sigmoid(gates.astype(jnp.float32)).reshape(b, t, N_BRANCH, D_MODEL)
    merged = None
    for i, o_i in enumerate((o_a, o_b, o_c, o_d)):
        term = gate[:, :, i] * (o_i @ wl['w_branch'][i])
        merged = term if merged is None else merged + term
    x = x + merged @ wl['w_out']
    x = x + mem_attend(rms_norm(x, wl['norm_mem']), mem_kv_l, wl['mem_w_q'], wl['mem_q_norm'], wl['mem_w_o'])
    x = x + hier_moe(rms_norm(x, wl['norm_moe']), wl['moe_w_coarse'], wl['moe_b_coarse'], wl['moe_w_fine'],
                     wl['moe_b_fine'], wl['moe_w_gate'], wl['moe_w_up'], wl['moe_w_down'])
    return x, gla_s, gdn_s, conv_new, s5_s, win_new


def setup_inputs(seed: int = 0) -> dict:
    key = jax.random.key(seed)
    keys = jax.random.split(key, 64)
    count = [0]

    def nxt():
        count[0] += 1
        return keys[count[0] - 1]

    def nrm(shape, scale=1.0):
        return scale * jax.random.normal(nxt(), shape, jnp.float32)

    def gain(shape):
        return 1.0 + 0.05 * jax.random.normal(nxt(), shape, jnp.float32)

    def unif(shape, lo, hi):
        return jax.random.uniform(nxt(), shape, jnp.float32, lo, hi)

    L = DEPTH
    win_len = tuple(min(w, PAST_LEN) for w, _ in DSA_PATTERNS)
    dt0 = jnp.exp(unif((L, GDN_HEADS), math.log(GDN_DT_MIN), math.log(GDN_DT_MAX)))
    s5_n = jnp.arange(S5_STATE, dtype=jnp.float32)
    kvh = (2, DSA_HEADS_PER_GROUP, DSA_HEAD_DIM)
    return {
        'x_prompt': nrm((BATCH, SEQ, D_MODEL)),
        'x_sample': nrm((DEC_BATCH, DEC_SEQ, D_MODEL)),
        'state_gla': nrm((L, DEC_BATCH, GLA_HEADS, GLA_DK, GLA_DV), 0.5),
        'state_gdn': nrm((L, DEC_BATCH, GDN_HEADS, GDN_DK, GDN_DV), 0.1),
        'state_gdn_conv': nrm((L, DEC_BATCH, GDN_CONV - 1, 3 * MIX_W)),
        'state_s5': nrm((L, DEC_BATCH, S5_GROUPS, S5_STATE, 2), 0.1),
        'cache_win128': nrm((L, DEC_BATCH, win_len[0]) + kvh),
        'cache_win512': nrm((L, DEC_BATCH, win_len[1]) + kvh),
        'cache_win2048': nrm((L, DEC_BATCH, win_len[2]) + kvh),
        'cache_mem': nrm((L, DEC_BATCH, MEM_TOKENS, 2, MEM_HEADS, MEM_HEAD_DIM)),
        'mem_prompt': nrm((BATCH, MEM_TOKENS, D_MODEL)),
        'norm_mix': gain((L, D_MODEL)),
        'norm_mem': gain((L, D_MODEL)),
        'norm_mem_src': gain((L, D_MODEL)),
        'norm_moe': gain((L, D_MODEL)),
        'w_in': nrm((L, D_MODEL, IN_COLS), D_MODEL ** -0.5),
        'gla_w_a2': nrm((L, GLA_RANK, GLA_HEADS * GLA_DK), GLA_RANK ** -0.5),
        'gla_b_a2': nrm((L, GLA_HEADS * GLA_DK), 0.01),
        'gla_norm': gain((L, GLA_DV)),
        'gdn_conv': nrm((L, GDN_CONV, 3 * MIX_W), GDN_CONV ** -0.5),
        'gdn_a_log': jnp.log(unif((L, GDN_HEADS), 1.0, 16.0)),
        'gdn_dt_bias': dt0 + jnp.log(-jnp.expm1(-dt0)),
        'gdn_norm': gain((L, GDN_DV)),
        's5_a_re': -0.5 + 0.01 * nrm((L, S5_GROUPS, S5_STATE)),
        's5_a_im': math.pi * s5_n + 0.01 * nrm((L, S5_GROUPS, S5_STATE)),
        's5_log_dt': unif((L, S5_GROUPS), math.log(S5_DT_MIN), math.log(S5_DT_MAX)),
        's5_b_re': nrm((L, S5_GROUPS, S5_STATE, S5_GROUP), (2 * S5_GROUP) ** -0.5),
        's5_b_im': nrm((L, S5_GROUPS, S5_STATE, S5_GROUP), (2 * S5_GROUP) ** -0.5),
        's5_c_re': nrm((L, S5_GROUPS, S5_GROUP, S5_STATE), S5_STATE ** -0.5),
        's5_c_im': nrm((L, S5_GROUPS, S5_GROUP, S5_STATE), S5_STATE ** -0.5),
        's5_d': nrm((L, MIX_W)),
        's5_w_glu': nrm((L, MIX_W, MIX_W), MIX_W ** -0.5),
        's5_b_glu': nrm((L, MIX_W), 0.01),
        'dsa_q_norm': gain((L, DSA_HEAD_DIM)),
        'dsa_k_norm': gain((L, DSA_HEAD_DIM)),
        'w_branch': nrm((L, N_BRANCH, MIX_W, D_MODEL), MIX_W ** -0.5),
        'w_out': nrm((L, D_MODEL, D_MODEL), D_MODEL ** -0.5),
        'mem_w_q': nrm((L, D_MODEL, MEM_HEADS * MEM_HEAD_DIM), D_MODEL ** -0.5),
        'mem_w_kv': nrm((L, D_MODEL, 2 * MEM_HEADS * MEM_HEAD_DIM), D_MODEL ** -0.5),
        'mem_q_norm': gain((L, MEM_HEAD_DIM)),
        'mem_k_norm': gain((L, MEM_HEAD_DIM)),
        'mem_w_o': nrm((L, MEM_HEADS * MEM_HEAD_DIM, D_MODEL), (MEM_HEADS * MEM_HEAD_DIM) ** -0.5),
        'moe_w_coarse': nrm((L, D_MODEL, MOE_GROUPS), D_MODEL ** -0.5),
        'moe_b_coarse': nrm((L, MOE_GROUPS), 0.01),
        'moe_w_fine': nrm((L, D_MODEL, MOE_EXPERTS), D_MODEL ** -0.5),
        'moe_b_fine': nrm((L, MOE_EXPERTS), 0.01),
        'moe_w_gate': nrm((L, MOE_EXPERTS, D_MODEL, MOE_FF), D_MODEL ** -0.5),
        'moe_w_up': nrm((L, MOE_EXPERTS, D_MODEL, MOE_FF), D_MODEL ** -0.5),
        'moe_w_down': nrm((L, MOE_EXPERTS, MOE_FF, D_MODEL), MOE_FF ** -0.5),
    }


def reference(x_prompt, x_sample, state_gla, state_gdn, state_gdn_conv, state_s5,
              cache_win128, cache_win512, cache_win2048, cache_mem, mem_prompt,
              norm_mix, norm_mem, norm_mem_src, norm_moe, w_in,
              gla_w_a2, gla_b_a2, gla_norm,
              gdn_conv, gdn_a_log, gdn_dt_bias, gdn_norm,
              s5_a_re, s5_a_im, s5_log_dt, s5_b_re, s5_b_im, s5_c_re, s5_c_im, s5_d, s5_w_glu, s5_b_glu,
              dsa_q_norm, dsa_k_norm, w_branch, w_out,
              mem_w_q, mem_w_kv, mem_q_norm, mem_k_norm, mem_w_o,
              moe_w_coarse, moe_b_coarse, moe_w_fine, moe_b_fine, moe_w_gate, moe_w_up, moe_w_down):
    weights = dict(
        norm_mix=norm_mix, norm_mem=norm_mem, norm_mem_src=norm_mem_src, norm_moe=norm_moe, w_in=w_in,
        gla_w_a2=gla_w_a2, gla_b_a2=gla_b_a2, gla_norm=gla_norm,
        gdn_conv=gdn_conv, gdn_a_log=gdn_a_log, gdn_dt_bias=gdn_dt_bias, gdn_norm=gdn_norm,
        s5_a_re=s5_a_re, s5_a_im=s5_a_im, s5_log_dt=s5_log_dt, s5_b_re=s5_b_re, s5_b_im=s5_b_im,
        s5_c_re=s5_c_re, s5_c_im=s5_c_im, s5_d=s5_d, s5_w_glu=s5_w_glu, s5_b_glu=s5_b_glu,
        dsa_q_norm=dsa_q_norm, dsa_k_norm=dsa_k_norm, w_branch=w_branch, w_out=w_out,
        mem_w_q=mem_w_q, mem_w_kv=mem_w_kv, mem_q_norm=mem_q_norm, mem_k_norm=mem_k_norm, mem_w_o=mem_w_o,
        moe_w_coarse=moe_w_coarse, moe_b_coarse=moe_b_coarse, moe_w_fine=moe_w_fine, moe_b_fine=moe_b_fine,
        moe_w_gate=moe_w_gate, moe_w_up=moe_w_up, moe_w_down=moe_w_down)
    bp = x_prompt.shape[0]
    yp, ys = x_prompt, x_sample
    gla_p, gla_s, gdn_p, gdn_s, conv_p, conv_s, s5_p, s5_s, mem_p = [], [], [], [], [], [], [], [], []
    win_p = [[] for _ in DSA_PATTERNS]
    win_s = [[] for _ in DSA_PATTERNS]
    for l in range(DEPTH):
        wl = {name: arr[l] for name, arr in weights.items()}
        kv_p = mem_kv(mem_prompt, wl['norm_mem_src'], wl['mem_w_kv'], wl['mem_k_norm'])
        yp, sa, sb, sc, sd, sw = decoder_layer(
            yp, wl,
            jnp.zeros((bp, GLA_HEADS, GLA_DK, GLA_DV), jnp.float32),
            jnp.zeros((bp, GDN_HEADS, GDN_DK, GDN_DV), jnp.float32),
            jnp.zeros((bp, GDN_CONV - 1, 3 * MIX_W), yp.dtype),
            jnp.zeros((bp, S5_GROUPS, S5_STATE, 2), jnp.float32),
            None, kv_p)
        gla_p.append(sa); gdn_p.append(sb); conv_p.append(sc); s5_p.append(sd); mem_p.append(kv_p)
        for gi in range(DSA_GROUPS):
            win_p[gi].append(sw[gi])
        ys, ta, tb, tc, td, tw = decoder_layer(
            ys, wl, state_gla[l], state_gdn[l], state_gdn_conv[l], state_s5[l],
            (cache_win128[l], cache_win512[l], cache_win2048[l]), cache_mem[l])
        gla_s.append(ta); gdn_s.append(tb); conv_s.append(tc); s5_s.append(td)
        for gi in range(DSA_GROUPS):
            win_s[gi].append(tw[gi])
    return (yp, ys,
            jnp.stack(gla_p), jnp.stack(gla_s),
            jnp.stack(gdn_p), jnp.stack(gdn_s),
            jnp.stack(conv_p), jnp.stack(conv_s),
            jnp.stack(s5_p), jnp.stack(s5_s),
            jnp.stack(win_p[0]), jnp.stack(win_s[0]),
            jnp.stack(win_p[1]), jnp.stack(win_s[1]),
            jnp.stack(win_p[2]), jnp.stack(win_s[2]),
            jnp.stack(mem_p))
```

```python
import functools
import math

import jax
import jax.numpy as jnp
from jax import lax
from jax.experimental import pallas as pl
from jax.experimental.pallas import tpu as pltpu

F32 = jnp.float32
BF = jnp.bfloat16
HIGHEST = lax.Precision.HIGHEST
NEG_INF = float("-inf")

EPS = 1e-6
D_MODEL = 4096
MIX_W = 1024
GLA_HEADS, GLA_DK, GLA_DV, GLA_RANK = 4, 128, 256, 16
GDN_HEADS, GDN_DK, GDN_DV, GDN_CONV = 8, 128, 128, 4
S5_GROUP, S5_GROUPS, S5_STATE = 16, 64, 64
DSA_HD, DSA_HEADS = 128, 8
DSA_PATTERNS = ((128, 1), (512, 4), (2048, 16))
DSA_SUB = 128
MEM_TOKENS, MEM_HEADS, MEM_HD = 256, 4, 128
MOE_GROUPS, MOE_PER_GROUP, MOE_EXPERTS, MOE_FF, MOE_BLOCK = 4, 8, 32, 1024, 128

LANE = 128
SUBLANE = 8
VMEM_LIMIT = 56 * 1024 * 1024
CHUNK = 128
SUBCHUNK = 16

OFF_AQ, OFF_AK, OFF_AV, OFF_AR = 0, 512, 1024, 2048
OFF_BQ, OFF_BK, OFF_BV, OFF_BZ = 3072, 4096, 5120, 6144
OFF_CU = 7168
OFF_DQ, OFF_DK, OFF_DV = 8192, 11264, 14336
OFF_GATE = 17408
OFF_ALR, OFF_BBA = 33792, 33920
N_COLS = 34816


def _cparams(sem):
    return pltpu.CompilerParams(dimension_semantics=sem, vmem_limit_bytes=VMEM_LIMIT)


def _tile(m, cap, mult=16):
    best = None
    for d in range(mult, min(m, cap) + 1, mult):
        if m % d == 0:
            best = d
    assert best is not None, (m, cap, mult)
    return best


def _bdot(a, b):
    return jnp.dot(a.astype(BF), b.astype(BF), preferred_element_type=F32)


def _bdot_nt(a, b):
    return lax.dot_general(a.astype(BF), b.astype(BF), (((1,), (1,)), ((), ())), preferred_element_type=F32)


def _bdot_tn(a, b):
    return lax.dot_general(a.astype(BF), b.astype(BF), (((0,), (0,)), ((), ())), preferred_element_type=F32)


def _row_rms(x, gain):
    return x * lax.rsqrt(jnp.mean(x * x, axis=-1, keepdims=True) + EPS) * gain


def _rmsnorm_body(x_ref, g_ref, o_ref):
    o_ref[...] = _row_rms(x_ref[...], g_ref[...]).astype(o_ref.dtype)


def rmsnorm(x, gain, out_dtype=BF):
    m, d = x.shape
    tm = _tile(m, 512)
    return pl.pallas_call(
        _rmsnorm_body, grid=(m // tm,),
        in_specs=[pl.BlockSpec((tm, d), lambda i: (i, 0)), pl.BlockSpec((1, d), lambda i: (0, 0))],
        out_specs=pl.BlockSpec((tm, d), lambda i: (i, 0)),
        out_shape=jax.ShapeDtypeStruct((m, d), out_dtype),
        compiler_params=_cparams(("parallel",)), name="rmsnorm")(x, gain.reshape(1, d))


def _mm_body(x_ref, w_ref, *rest):
    o_ref = rest[-1]
    acc = _bdot(x_ref[...], w_ref[...])
    if len(rest) == 2:
        acc = acc + rest[0][...]
    o_ref[...] = acc.astype(o_ref.dtype)


def matmul(x, w, residual=None, out_dtype=F32, tm_cap=1024, tn=512):
    m, k = x.shape
    n = w.shape[1]
    tm = _tile(m, tm_cap)
    assert n % tn == 0
    in_specs = [pl.BlockSpec((tm, k), lambda i, j: (i, 0)), pl.BlockSpec((k, tn), lambda i, j: (0, j))]
    args = [x, w]
    if residual is not None:
        in_specs.append(pl.BlockSpec((tm, tn), lambda i, j: (i, j)))
        args.append(residual)
    return pl.pallas_call(
        _mm_body, grid=(m // tm, n // tn), in_specs=in_specs,
        out_specs=pl.BlockSpec((tm, tn), lambda i, j: (i, j)),
        out_shape=jax.ShapeDtypeStruct((m, n), out_dtype),
        compiler_params=_cparams(("parallel", "parallel")), name="matmul")(*args)


def _gla_body(q_ref, k_ref, v_ref, lr_ref, r_ref, wa_ref, ba_ref, gn_ref, s0_ref, o_ref, so_ref,
              st_scr, kh_scr, bh_scr, vh_scr, att_scr, *, t_valid):
    C, c = CHUNK, SUBCHUNK
    t = pl.program_id(2)

    @pl.when(t == 0)
    def _():
        st_scr[...] = s0_ref[0, 0]

    row = lax.broadcasted_iota(jnp.int32, (C, 1), 0)
    valid = (t * C + row) < t_valid
    q = jnp.where(valid, q_ref[...] * (GLA_DK ** -0.5), 0.0)
    k = jnp.where(valid, k_ref[...], 0.0)
    v = jnp.where(valid, v_ref[...], 0.0)
    z = jnp.dot(lr_ref[...], wa_ref[...], precision=HIGHEST, preferred_element_type=F32) + ba_ref[...]
    g = jnp.where(valid, jax.nn.log_sigmoid(z) * (1.0 / 16.0), 0.0)
    tri = (lax.broadcasted_iota(jnp.int32, (C, C), 0) >= lax.broadcasted_iota(jnp.int32, (C, C), 1)).astype(F32)
    bb = jnp.dot(tri, g, precision=HIGHEST, preferred_element_type=F32)

    zero_c = jnp.zeros((c, GLA_DK), F32)
    kh_scr[0:c, :] = zero_c
    bh_scr[0:c, :] = zero_c
    vh_scr[0:c, :] = jnp.zeros((c, GLA_DV), F32)
    kh_scr[c:c + C, :] = k
    bh_scr[c:c + C, :] = bb
    vh_scr[c:c + C, :] = v
    sub = row % c
    o = jnp.zeros((C, GLA_DV), F32)
    for j in range(c):
        ks = kh_scr[c - j:c - j + C, :]
        bs = bh_scr[c - j:c - j + C, :]
        vs = vh_scr[c - j:c - j + C, :]
        dec = jnp.exp(jnp.where(sub >= j, bb - bs, NEG_INF))
        o = o + jnp.sum(q * ks * dec, axis=1, keepdims=True) * vs

    att_scr[0:c, :] = jnp.zeros((c, C), F32)
    for i in range(1, C // c):
        b_i = bb[i * c - 1:i * c, :]
        lhs = q[i * c:(i + 1) * c, :] * jnp.exp(bb[i * c:(i + 1) * c, :] - b_i)
        rhs = k * jnp.exp(jnp.where(row < i * c, b_i - bb, NEG_INF))
        att_scr[i * c:(i + 1) * c, :] = _bdot_nt(lhs, rhs)
    o = o + _bdot(att_scr[...], v)

    st = st_scr[...]
    o = o + _bdot_nt(q * jnp.exp(bb), st)
    b_last = bb[C - 1:C, :]
    st_new = st * jnp.exp(b_last) + _bdot_tn(v, k * jnp.exp(b_last - bb))
    st_scr[...] = st_new

    @pl.when(t == pl.num_programs(2) - 1)
    def _():
        so_ref[0, 0] = st_new

    r = r_ref[...]
    o_ref[...] = (_row_rms(o, gn_ref[...]) * (r * jax.nn.sigmoid(r))).astype(o_ref.dtype)


def gla_mixer(p, off_alr, row0, nb, t_len, t_valid, wa_pad, ba, gnorm, s0_t):
    C = CHUNK
    nt = t_len // C
    rb0 = row0 // C

    def rmap(off, width):
        return lambda b, h, t: (rb0 + b * nt + t, off // width + h)

    in_specs = [
        pl.BlockSpec((C, GLA_DK), rmap(OFF_AQ, GLA_DK)),
        pl.BlockSpec((C, GLA_DK), rmap(OFF_AK, GLA_DK)),
        pl.BlockSpec((C, GLA_DV), rmap(OFF_AV, GLA_DV)),
        pl.BlockSpec((C, LANE), lambda b, h, t: (rb0 + b * nt + t, off_alr // LANE)),
        pl.BlockSpec((C, GLA_DV), rmap(OFF_AR, GLA_DV)),
        pl.BlockSpec((LANE, GLA_DK), lambda b, h, t: (0, h)),
        pl.BlockSpec((1, GLA_DK), lambda b, h, t: (0, h)),
        pl.BlockSpec((1, GLA_DV), lambda b, h, t: (0, 0)),
        pl.BlockSpec((1, 1, GLA_DV, GLA_DK), lambda b, h, t: (b, h, 0, 0)),
    ]
    out_specs = [
        pl.BlockSpec((C, GLA_DV), lambda b, h, t: (b * nt + t, h)),
        pl.BlockSpec((1, 1, GLA_DV, GLA_DK), lambda b, h, t: (b, h, 0, 0)),
    ]
    return pl.pallas_call(
        functools.partial(_gla_body, t_valid=t_valid),
        grid=(nb, GLA_HEADS, nt), in_specs=in_specs, out_specs=out_specs,
        out_shape=[jax.ShapeDtypeStruct((nb * t_len, MIX_W), BF),
                   jax.ShapeDtypeStruct((nb, GLA_HEADS, GLA_DV, GLA_DK), F32)],
        scratch_shapes=[pltpu.VMEM((GLA_DV, GLA_DK), F32),
                        pltpu.VMEM((C + SUBCHUNK, GLA_DK), F32), pltpu.VMEM((C + SUBCHUNK, GLA_DK), F32),
                        pltpu.VMEM((C + SUBCHUNK, GLA_DV), F32), pltpu.VMEM((C, C), F32)],
        compiler_params=_cparams(("parallel", "parallel", "arbitrary")), name="gla")(
            p, p, p, p, p, wa_pad, ba, gnorm, s0_t)


def _gdn_body(xq_ref, xk_ref, xv_ref, z_ref, ba_ref, cq_ref, ck_ref, cv_ref, wq_ref, wk_ref, wv_ref,
              alog_ref, dtb_ref, gn_ref, s0_ref, o_ref, so_ref, s_scr, q_scr, k_scr, v_scr, *, t_valid):
    C, halo = CHUNK, SUBLANE
    h = pl.program_id(1)
    t = pl.program_id(2)

    @pl.when(t == 0)
    def _():
        s_scr[...] = s0_ref[0, 0]
        q_scr[0:halo, :] = cq_ref[0]
        k_scr[0:halo, :] = ck_ref[0]
        v_scr[0:halo, :] = cv_ref[0]

    def conv_silu(x_ref, w_ref, scr):
        scr[halo:halo + C, :] = x_ref[...]
        w = w_ref[...]
        acc = jnp.zeros((C, LANE), F32)
        for i in range(GDN_CONV):
            lo = halo - (GDN_CONV - 1) + i
            acc = acc + scr[lo:lo + C, :] * w[i:i + 1, :]
        scr[0:halo, :] = scr[C:C + halo, :]
        return acc * jax.nn.sigmoid(acc)

    row = lax.broadcasted_iota(jnp.int32, (C, 1), 0)
    valid = (t * C + row) < t_valid
    q = conv_silu(xq_ref, wq_ref, q_scr)
    k = conv_silu(xk_ref, wk_ref, k_scr)
    v = conv_silu(xv_ref, wv_ref, v_scr)
    q = q * lax.rsqrt(jnp.sum(q * q, axis=-1, keepdims=True) + EPS) * (GDN_DK ** -0.5)
    k = k * lax.rsqrt(jnp.sum(k * k, axis=-1, keepdims=True) + EPS)
    q = jnp.where(valid, q, 0.0)
    k = jnp.where(valid, k, 0.0)
    v = jnp.where(valid, v, 0.0)

    ba = ba_ref[...]
    lane = lax.broadcasted_iota(jnp.int32, (C, LANE), 1)
    beta = jnp.sum(jnp.where(lane == h, jax.nn.sigmoid(ba), 0.0), axis=1, keepdims=True)
    g_all = -jnp.exp(alog_ref[...]) * jax.nn.softplus(ba + dtb_ref[...])
    g = jnp.sum(jnp.where(lane == h + GDN_HEADS, g_all, 0.0), axis=1, keepdims=True)
    beta = jnp.where(valid, beta, 0.0)
    g = jnp.where(valid, g, 0.0)

    ii = lax.broadcasted_iota(jnp.int32, (C, C), 0)
    jj = lax.broadcasted_iota(jnp.int32, (C, C), 1)
    tri = (ii >= jj).astype(F32)
    gc_b = jnp.dot(tri, jnp.broadcast_to(g, (C, LANE)), precision=HIGHEST, preferred_element_type=F32)
    gc = gc_b[:, 0:1]
    first = (lax.broadcasted_iota(jnp.int32, (C, LANE), 1) == 0).astype(F32)
    gc_row = lax.dot_general(first, gc_b, (((1,), (1,)), ((), ())), precision=HIGHEST,
                             preferred_element_type=F32)
    diff = gc - gc_row

    a = beta * _bdot_nt(k, k) * jnp.exp(jnp.where(ii > jj, diff, NEG_INF))
    u = beta * v
    w = beta * k * jnp.exp(gc)
    for j in range(C - 1):
        col = a[:, j:j + 1]
        u = u - col * u[j:j + 1, :]
        w = w - col * w[j:j + 1, :]

    qk = _bdot_nt(q, k) * jnp.exp(jnp.where(ii >= jj, diff, NEG_INF))
    s = s_scr[...]
    delta = u - _bdot(w, s)
    o = _bdot(q * jnp.exp(gc), s) + _bdot(qk, delta)
    g_last = gc[C - 1:C, :]
    s_new = jnp.exp(g_last) * s + _bdot_tn(k * jnp.exp(g_last - gc), delta)
    s_scr[...] = s_new

    @pl.when(t == pl.num_programs(2) - 1)
    def _():
        so_ref[0, 0] = s_new

    zg = z_ref[...]
    o_ref[...] = (_row_rms(o, gn_ref[...]) * (zg * jax.nn.sigmoid(zg))).astype(o_ref.dtype)


def gdn_mixer(p, off_bba, row0, nb, t_len, t_valid, conv_pad, conv_w, alog_pad, dtb_pad, gnorm, s0):
    C = CHUNK
    nt = t_len // C
    rb0 = row0 // C

    def rmap(off):
        return lambda b, h, t: (rb0 + b * nt + t, off // LANE + h)

    def cmap(part):
        return lambda b, h, t: (b, 0, part * GDN_HEADS + h)

    def wmap(part):
        return lambda b, h, t: (0, part * GDN_HEADS + h)

    const = lambda b, h, t: (0, 0)
    in_specs = [
        pl.BlockSpec((C, LANE), rmap(OFF_BQ)), pl.BlockSpec((C, LANE), rmap(OFF_BK)),
        pl.BlockSpec((C, LANE), rmap(OFF_BV)), pl.BlockSpec((C, LANE), rmap(OFF_BZ)),
        pl.BlockSpec((C, LANE), lambda b, h, t: (rb0 + b * nt + t, off_bba // LANE)),
        pl.BlockSpec((1, SUBLANE, LANE), cmap(0)), pl.BlockSpec((1, SUBLANE, LANE), cmap(1)),
        pl.BlockSpec((1, SUBLANE, LANE), cmap(2)),
        pl.BlockSpec((GDN_CONV, LANE), wmap(0)), pl.BlockSpec((GDN_CONV, LANE), wmap(1)),
        pl.BlockSpec((GDN_CONV, LANE), wmap(2)),
        pl.BlockSpec((1, LANE), const), pl.BlockSpec((1, LANE), const), pl.BlockSpec((1, GDN_DV), const),
        pl.BlockSpec((1, 1, GDN_DK, GDN_DV), lambda b, h, t: (b, h, 0, 0)),
    ]
    out_specs = [
        pl.BlockSpec((C, GDN_DV), lambda b, h, t: (b * nt + t, h)),
        pl.BlockSpec((1, 1, GDN_DK, GDN_DV), lambda b, h, t: (b, h, 0, 0)),
    ]
    return pl.pallas_call(
        functools.partial(_gdn_body, t_valid=t_valid),
        grid=(nb, GDN_HEADS, nt), in_specs=in_specs, out_specs=out_specs,
        out_shape=[jax.ShapeDtypeStruct((nb * t_len, MIX_W), BF),
                   jax.ShapeDtypeStruct((nb, GDN_HEADS, GDN_DK, GDN_DV), F32)],
        scratch_shapes=[pltpu.VMEM((GDN_DK, GDN_DV), F32)] + [pltpu.VMEM((C + SUBLANE, LANE), F32)] * 3,
        compiler_params=_cparams(("parallel", "parallel", "arbitrary")), name="gdn")(
            p, p, p, p, p, conv_pad, conv_pad, conv_pad, conv_w, conv_w, conv_w,
            alog_pad, dtb_pad, gnorm, s0)


S5_NBLK = 4
S5_UW = MIX_W // S5_NBLK
S5_SW = S5_GROUPS * S5_STATE // S5_NBLK


def _s5_body(u_ref, wbr_ref, wbi_ref, wcr_ref, wci_ref, ar_ref, ai_ref, h0r_ref, h0i_ref, d_ref,
             wg_ref, bg_ref, o_ref, hlr_ref, hli_ref, bur, bui, hcr, hci, y_scr, *, tb):
    t = pl.program_id(1)

    @pl.when(t == 0)
    def _():
        hcr[...] = h0r_ref[0]
        hci[...] = h0i_ref[0]

    u = u_ref[...]
    for m in range(S5_NBLK):
        um = u[:, m * S5_UW:(m + 1) * S5_UW]
        bur[:, m * S5_SW:(m + 1) * S5_SW] = _bdot(um, wbr_ref[m])
        bui[:, m * S5_SW:(m + 1) * S5_SW] = _bdot(um, wbi_ref[m])

    for m in range(S5_NBLK):
        cols = pl.ds(m * S5_SW, S5_SW)
        a_r = ar_ref[:, cols]
        a_i = ai_ref[:, cols]

        def step(i, carry):
            h_r, h_i = carry
            n_r = a_r * h_r - a_i * h_i + bur[pl.ds(i, 1), cols]
            n_i = a_r * h_i + a_i * h_r + bui[pl.ds(i, 1), cols]
            bur[pl.ds(i, 1), cols] = n_r
            bui[pl.ds(i, 1), cols] = n_i
            return n_r, n_i

        h_r, h_i = lax.fori_loop(0, tb, step, (hcr[:, cols], hci[:, cols]))
        hcr[:, cols] = h_r
        hci[:, cols] = h_i
        y_scr[:, m * S5_UW:(m + 1) * S5_UW] = (_bdot(bur[:, cols], wcr_ref[m]) - _bdot(bui[:, cols], wci_ref[m]))

    @pl.when(t == pl.num_programs(1) - 1)
    def _():
        hlr_ref[0] = hcr[...]
        hli_ref[0] = hci[...]

    y = jax.nn.gelu(y_scr[...] + d_ref[...] * u)
    o_ref[...] = (y * jax.nn.sigmoid(_bdot(y, wg_ref[...]) + bg_ref[...])).astype(o_ref.dtype)


def s5_mixer(p, row0, nb, t_len, tb, wbr, wbi, wcr, wci, abar_r, abar_i, h0r, h0i, dskip, wglu, bglu):
    nt = t_len // tb
    rb0 = row0 // tb
    nst = S5_GROUPS * S5_STATE
    c2 = lambda b, t: (0, 0)
    c3 = lambda b, t: (0, 0, 0)
    in_specs = [
        pl.BlockSpec((tb, MIX_W), lambda b, t: (rb0 + b * nt + t, OFF_CU // MIX_W)),
        pl.BlockSpec((S5_NBLK, S5_UW, S5_SW), c3), pl.BlockSpec((S5_NBLK, S5_UW, S5_SW), c3),
        pl.BlockSpec((S5_NBLK, S5_SW, S5_UW), c3), pl.BlockSpec((S5_NBLK, S5_SW, S5_UW), c3),
        pl.BlockSpec((1, nst), c2), pl.BlockSpec((1, nst), c2),
        pl.BlockSpec((1, 1, nst), lambda b, t: (b, 0, 0)), pl.BlockSpec((1, 1, nst), lambda b, t: (b, 0, 0)),
        pl.BlockSpec((1, MIX_W), c2), pl.BlockSpec((MIX_W, MIX_W), c2), pl.BlockSpec((1, MIX_W), c2),
    ]
    out_specs = [
        pl.BlockSpec((tb, MIX_W), lambda b, t: (b * nt + t, 0)),
        pl.BlockSpec((1, 1, nst), lambda b, t: (b, 0, 0)), pl.BlockSpec((1, 1, nst), lambda b, t: (b, 0, 0)),
    ]
    return pl.pallas_call(
        functools.partial(_s5_body, tb=tb), grid=(nb, nt), in_specs=in_specs, out_specs=out_specs,
        out_shape=[jax.ShapeDtypeStruct((nb * t_len, MIX_W), BF),
                   jax.ShapeDtypeStruct((nb, 1, nst), F32), jax.ShapeDtypeStruct((nb, 1, nst), F32)],
        scratch_shapes=[pltpu.VMEM((tb, nst), F32), pltpu.VMEM((tb, nst), F32),
                        pltpu.VMEM((1, nst), F32), pltpu.VMEM((1, nst), F32), pltpu.VMEM((tb, MIX_W), F32)],
        compiler_params=_cparams(("parallel", "arbitrary")), name="s5")(
            p, wbr, wbi, wcr, wci, abar_r, abar_i, h0r, h0i, dskip, wglu, bglu)


def _s5_discretise_body(are_ref, aim_ref, ldt_ref, abr_ref, abi_ref, fr_ref, fi_ref):
    a_r, a_i = are_ref[...], aim_ref[...]
    dt = jnp.exp(ldt_ref[...])
    mag = jnp.exp(a_r * dt)
    ab_r, ab_i = mag * jnp.cos(a_i * dt), mag * jnp.sin(a_i * dt)
    den = a_r * a_r + a_i * a_i
    abr_ref[...] = ab_r
    abi_ref[...] = ab_i
    fr_ref[...] = ((ab_r - 1.0) * a_r + ab_i * a_i) / den
    fi_ref[...] = (ab_i * a_r - (ab_r - 1.0) * a_i) / den


def s5_discretise(a_re, a_im, log_dt):
    g, n = a_re.shape
    ldt = jnp.broadcast_to(log_dt[:, None], (g, n))
    sds = jax.ShapeDtypeStruct((g, n), F32)
    return pl.pallas_call(_s5_discretise_body, out_shape=[sds] * 4, name="s5_discretise")(a_re, a_im, ldt)


def _dsa_prompt_body(q_ref, kc_ref, kp_ref, vc_ref, vp_ref, qn_ref, kn_ref, o_ref, lse_ref, ko_ref):
    c = pl.program_id(2)
    n = DSA_SUB
    ii = lax.broadcasted_iota(jnp.int32, (n, n), 0)
    jj = lax.broadcasted_iota(jnp.int32, (n, n), 1)
    cur_ok = jj <= ii
    prev_ok = jnp.logical_and(jj >= ii, c > 0)
    qg, kg = qn_ref[...], kn_ref[...]
    for h in range(DSA_HEADS):
        sl = slice(h * DSA_HD, (h + 1) * DSA_HD)
        q = _row_rms(q_ref[:, sl], qg)
        kc = _row_rms(kc_ref[:, sl], kg)
        kp = _row_rms(kp_ref[:, sl], kg)
        s_c = jnp.where(cur_ok, _bdot_nt(q, kc) * (DSA_HD ** -0.5), NEG_INF)
        s_p = jnp.where(prev_ok, _bdot_nt(q, kp) * (DSA_HD ** -0.5), NEG_INF)
        mx = jnp.maximum(jnp.max(s_c, axis=1, keepdims=True), jnp.max(s_p, axis=1, keepdims=True))
        p_c = jnp.exp(s_c - mx)
        p_p = jnp.exp(s_p - mx)
        den = jnp.sum(p_c, axis=1, keepdims=True) + jnp.sum(p_p, axis=1, keepdims=True)
        o = (_bdot(p_c, vc_ref[:, sl]) + _bdot(p_p, vp_ref[:, sl])) / den
        o_ref[:, sl] = o
        lse_ref[:, sl] = jnp.broadcast_to(mx + jnp.log(den), (n, DSA_HD))
        ko_ref[:, sl] = kc


def dsa_prompt(p, nb, t_len, gi, dil, q_norm, k_norm):
    n_rows, n_cols = p.shape
    assert n_rows % dil == 0 and t_len % (dil * DSA_SUB) == 0
    pv = p.reshape(n_rows // dil, dil * n_cols)
    nq = t_len // dil // DSA_SUB
    ncb = n_cols // MIX_W

    def cmap(off, back):
        return lambda b, r, c: (b * nq + jnp.maximum(c - back, 0), r * ncb + off // MIX_W + gi)

    blk = (DSA_SUB, MIX_W)
    in_specs = [pl.BlockSpec(blk, cmap(OFF_DQ, 0)), pl.BlockSpec(blk, cmap(OFF_DK, 0)),
                pl.BlockSpec(blk, cmap(OFF_DK, 1)), pl.BlockSpec(blk, cmap(OFF_DV, 0)),
                pl.BlockSpec(blk, cmap(OFF_DV, 1)),
                pl.BlockSpec((1, DSA_HD), lambda b, r, c: (0, 0)), pl.BlockSpec((1, DSA_HD), lambda b, r, c: (0, 0))]
    o_spec = pl.BlockSpec(blk, lambda b, r, c: (b * nq + c, r))
    sds = jax.ShapeDtypeStruct((nb * t_len // dil, dil * MIX_W), F32)
    outs = pl.pallas_call(
        _dsa_prompt_body, grid=(nb, dil, nq), in_specs=in_specs, out_specs=[o_spec] * 3, out_shape=[sds] * 3,
        compiler_params=_cparams(("parallel", "parallel", "arbitrary")), name=f"dsa_prompt{gi}")(
            pv, pv, pv, pv, pv, q_norm, k_norm)
    return [x.reshape(nb * t_len, MIX_W) for x in outs]


def _dsa_sample_body(q_ref, k_ref, v_ref, ck_ref, cv_ref, qn_ref, kn_ref, o_ref, lse_ref, co_ref, *, dil, t_new):
    s = pl.program_id(1)
    w = ck_ref.shape[1]

    @pl.when(s < DSA_HEADS)
    def _():
        q = _row_rms(q_ref[...], qn_ref[...])
        kn = _row_rms(k_ref[...], kn_ref[...])
        ck = ck_ref[0]
        tq = lax.broadcasted_iota(jnp.int32, (t_new, w), 0)
        d_c = w + tq - lax.broadcasted_iota(jnp.int32, (t_new, w), 1)
        ok_c = jnp.logical_and((d_c & (dil - 1)) == 0, d_c <= dil * DSA_SUB)
        d_n = lax.broadcasted_iota(jnp.int32, (t_new, t_new), 0) - lax.broadcasted_iota(jnp.int32, (t_new, t_new), 1)
        ok_n = jnp.logical_and(d_n >= 0, (d_n & (dil - 1)) == 0)
        s_c = jnp.where(ok_c, _bdot_nt(q, ck) * (DSA_HD ** -0.5), NEG_INF)
        s_n = jnp.where(ok_n, _bdot_nt(q, kn) * (DSA_HD ** -0.5), NEG_INF)
        mx = jnp.maximum(jnp.max(s_c, axis=1, keepdims=True), jnp.max(s_n, axis=1, keepdims=True))
        p_c = jnp.exp(s_c - mx)
        p_n = jnp.exp(s_n - mx)
        den = jnp.sum(p_c, axis=1, keepdims=True) + jnp.sum(p_n, axis=1, keepdims=True)
        o_ref[...] = (_bdot(p_c, cv_ref[0]) + _bdot(p_n, v_ref[...])) / den
        lse_ref[...] = jnp.broadcast_to(mx + jnp.log(den), (t_new, DSA_HD))
        co_ref[0, 0:w - t_new, :] = ck[t_new:w, :]
        co_ref[0, w - t_new:w, :] = kn

    @pl.when(s >= DSA_HEADS)
    def _():
        co_ref[0, 0:w - t_new, :] = cv_ref[0, t_new:w, :]
        co_ref[0, w - t_new:w, :] = v_ref[...]


def dsa_sample(p, row0, nb, t_new, gi, dil, cache, q_norm, k_norm):
    w = cache.shape[1]
    rb0 = row0 // t_new
    hq = lambda s: jnp.minimum(s, DSA_HEADS - 1)
    hv = lambda s: s % DSA_HEADS

    def pmap(off, hsel):
        return lambda b, s: (rb0 + b, (off + gi * MIX_W) // DSA_HD + hsel(s))

    in_specs = [
        pl.BlockSpec((t_new, DSA_HD), pmap(OFF_DQ, hq)), pl.BlockSpec((t_new, DSA_HD), pmap(OFF_DK, hq)),
        pl.BlockSpec((t_new, DSA_HD), pmap(OFF_DV, hv)),
        pl.BlockSpec((1, w, DSA_HD), lambda b, s: (b, 0, hq(s))),
        pl.BlockSpec((1, w, DSA_HD), lambda b, s: (b, 0, DSA_HEADS + hv(s))),
        pl.BlockSpec((1, DSA_HD), lambda b, s: (0, 0)), pl.BlockSpec((1, DSA_HD), lambda b, s: (0, 0)),
    ]
    out_specs = [pl.BlockSpec((t_new, DSA_HD), lambda b, s: (b, hq(s))),
                 pl.BlockSpec((t_new, DSA_HD), lambda b, s: (b, hq(s))),
                 pl.BlockSpec((1, w, DSA_HD), lambda b, s: (b, 0, s))]
    sds = jax.ShapeDtypeStruct((nb * t_new, MIX_W), F32)
    return pl.pallas_call(
        functools.partial(_dsa_sample_body, dil=dil, t_new=t_new),
        grid=(nb, 2 * DSA_HEADS), in_specs=in_specs, out_specs=out_specs,
        out_shape=[sds, sds, jax.ShapeDtypeStruct(cache.shape, F32)],
        compiler_params=_cparams(("parallel", "arbitrary")), name=f"dsa_sample{gi}")(
            p, p, p, cache, cache, q_norm, k_norm)


def _dsa_merge_body(o0, o1, o2, l0, l1, l2, out_ref):
    a, b, c = l0[...], l1[...], l2[...]
    mx = jnp.maximum(jnp.maximum(a, b), c)
    wa, wb, wc = jnp.exp(a - mx), jnp.exp(b - mx), jnp.exp(c - mx)
    out_ref[...] = ((wa * o0[...] + wb * o1[...] + wc * o2[...]) / (wa + wb + wc)).astype(out_ref.dtype)


def dsa_merge(outs, lses):
    m = outs[0].shape[0]
    tm = _tile(m, 512)
    spec = pl.BlockSpec((tm, MIX_W), lambda i: (i, 0))
    return pl.pallas_call(
        _dsa_merge_body, grid=(m // tm,), in_specs=[spec] * 6, out_specs=spec,
        out_shape=jax.ShapeDtypeStruct((m, MIX_W), BF),
        compiler_params=_cparams(("parallel",)), name="dsa_merge")(*outs, *lses)


def _branch_body(oa, ob, oc, od, wb_ref, ga, gb, gc, gd, out_ref):
    acc = None
    for i, (o_ref, g_ref) in enumerate(((oa, ga), (ob, gb), (oc, gc), (od, gd))):
        term = jax.nn.sigmoid(g_ref[...]) * _bdot(o_ref[...], wb_ref[i])
        acc = term if acc is None else acc + term
    out_ref[...] = acc.astype(out_ref.dtype)


def branch_merge(o_list, w_branch, p):
    m = p.shape[0]
    tm = _tile(m, 512)
    tn = 512
    o_spec = pl.BlockSpec((tm, MIX_W), lambda i, j: (i, 0))

    def gmap(b):
        return lambda i, j: (i, (OFF_GATE + b * D_MODEL) // tn + j)

    in_specs = [o_spec] * 4 + [pl.BlockSpec((4, MIX_W, tn), lambda i, j: (0, 0, j))] + \
        [pl.BlockSpec((tm, tn), gmap(b)) for b in range(4)]
    return pl.pallas_call(
        _branch_body, grid=(m // tm, D_MODEL // tn), in_specs=in_specs,
        out_specs=pl.BlockSpec((tm, tn), lambda i, j: (i, j)),
        out_shape=jax.ShapeDtypeStruct((m, D_MODEL), BF),
        compiler_params=_cparams(("parallel", "parallel")), name="branch_merge")(*o_list, w_branch, p, p, p, p)


def _kv_norm_body(kv_ref, g_ref, o_ref):
    half = MEM_HEADS * MEM_HD
    for h in range(MEM_HEADS):
        sl = slice(h * MEM_HD, (h + 1) * MEM_HD)
        o_ref[:, sl] = _row_rms(kv_ref[:, sl], g_ref[...])
    o_ref[:, half:] = kv_ref[:, half:]


def kv_norm(kv, k_norm):
    m, n = kv.shape
    tm = _tile(m, 512)
    return pl.pallas_call(
        _kv_norm_body, grid=(m // tm,),
        in_specs=[pl.BlockSpec((tm, n), lambda i: (i, 0)), pl.BlockSpec((1, MEM_HD), lambda i: (0, 0))],
        out_specs=pl.BlockSpec((tm, n), lambda i: (i, 0)), out_shape=jax.ShapeDtypeStruct((m, n), F32),
        compiler_params=_cparams(("parallel",)), name="kv_norm")(kv, k_norm)


def _mem_attn_body(q_ref, kv_ref, qn_ref, o_ref):
    half = MEM_HEADS * MEM_HD
    for h in range(MEM_HEADS):
        sl = slice(h * MEM_HD, (h + 1) * MEM_HD)
        q = _row_rms(q_ref[:, sl], qn_ref[...])
        s = _bdot_nt(q, kv_ref[0, :, sl]) * (MEM_HD ** -0.5)
        p = jnp.exp(s - jnp.max(s, axis=1, keepdims=True))
        o = _bdot(p, kv_ref[0, :, half + h * MEM_HD:half + (h + 1) * MEM_HD]) / jnp.sum(p, axis=1, keepdims=True)
        o_ref[:, sl] = o.astype(o_ref.dtype)


def mem_attend(q_all, row0, nb, t_len, tq, kv, q_norm):
    nt = t_len // tq
    rb0 = row0 // tq
    width = MEM_HEADS * MEM_HD
    return pl.pallas_call(
        _mem_attn_body, grid=(nb, nt),
        in_specs=[pl.BlockSpec((tq, width), lambda b, t: (rb0 + b * nt + t, 0)),
                  pl.BlockSpec((1, MEM_TOKENS, 2 * width), lambda b, t: (b, 0, 0)),
                  pl.BlockSpec((1, MEM_HD), lambda b, t: (0, 0))],
        out_specs=pl.BlockSpec((tq, width), lambda b, t: (b * nt + t, 0)),
        out_shape=jax.ShapeDtypeStruct((nb * t_len, width), BF),
        compiler_params=_cparams(("parallel", "parallel")), name="mem_attend")(q_all, kv, q_norm)


def _router_body(x_ref, g_ref, w_ref, b_ref, xn_ref, e_ref, wt_ref):
    xn = _row_rms(x_ref[...], g_ref[...])
    xn_ref[...] = xn.astype(xn_ref.dtype)
    logit = jnp.dot(xn, w_ref[...], precision=HIGHEST, preferred_element_type=F32) + b_ref[...]
    tm = logit.shape[0]
    lane = lax.broadcasted_iota(jnp.int32, (tm, LANE), 1)
    big = jnp.int32(LANE)
    lc = jnp.where(lane < MOE_GROUPS, logit, NEG_INF)
    mc = jnp.max(lc, axis=1, keepdims=True)
    grp = jnp.min(jnp.where(lc == mc, lane, big), axis=1, keepdims=True)
    p_grp = 1.0 / jnp.sum(jnp.exp(lc - mc), axis=1, keepdims=True)
    lo = MOE_GROUPS + grp * MOE_PER_GROUP
    lf = jnp.where(jnp.logical_and(lane >= lo, lane < lo + MOE_PER_GROUP), logit, NEG_INF)
    v1 = jnp.max(lf, axis=1, keepdims=True)
    i1 = jnp.min(jnp.where(lf == v1, lane, big), axis=1, keepdims=True)
    lf2 = jnp.where(lane == i1, NEG_INF, lf)
    v2 = jnp.max(lf2, axis=1, keepdims=True)
    i2 = jnp.min(jnp.where(lf2 == v2, lane, big), axis=1, keepdims=True)
    r = jnp.exp(v2 - v1)
    w1 = p_grp / (1.0 + r)
    w2 = p_grp * r / (1.0 + r)
    e_ref[...] = jnp.where(lane == 0, i1 - MOE_GROUPS, jnp.where(lane == 1, i2 - MOE_GROUPS, 0))
    wt_ref[...] = jnp.where(lane == 0, w1, jnp.where(lane == 1, w2, 0.0))


def moe_router(x, gain, w_route, b_route):
    m, d = x.shape
    tm = _tile(m, 384, mult=8)
    row = lambda i: (i, 0)
    fix = lambda i: (0, 0)
    return pl.pallas_call(
        _router_body, grid=(m // tm,),
        in_specs=[pl.BlockSpec((tm, d), row), pl.BlockSpec((1, d), fix), pl.BlockSpec((d, LANE), fix),
                  pl.BlockSpec((1, LANE), fix)],
        out_specs=[pl.BlockSpec((tm, d), row), pl.BlockSpec((tm, LANE), row), pl.BlockSpec((tm, LANE), row)],
        out_shape=[jax.ShapeDtypeStruct((m, d), BF), jax.ShapeDtypeStruct((m, LANE), jnp.int32),
                   jax.ShapeDtypeStruct((m, LANE), F32)],
        compiler_params=_cparams(("parallel",)), name="moe_router")(x, gain.reshape(1, d), w_route, b_route)


def _row_copy(src_hbm, dst_ref, src_row, dst_row, sem):
    return pltpu.make_async_copy(src_hbm.at[pl.ds(src_row, 1), :], dst_ref.at[pl.ds(dst_row, 1), :], sem)


def _gather_body(idx_ref, x_hbm, o_ref, sem):
    base = pl.program_id(0) * MOE_BLOCK

    def issue(r, carry):
        _row_copy(x_hbm, o_ref, idx_ref[base + r], r, sem).start()
        return carry

    def wait(r, carry):
        _row_copy(x_hbm, o_ref, 0, r, sem).wait()
        return carry

    lax.fori_loop(0, MOE_BLOCK, issue, 0)
    lax.fori_loop(0, MOE_BLOCK, wait, 0)


def moe_gather(x_words, src_row):
    n_rows = src_row.shape[0]
    width = x_words.shape[1]
    grid_spec = pltpu.PrefetchScalarGridSpec(
        num_scalar_prefetch=1, grid=(n_rows // MOE_BLOCK,),
        in_specs=[pl.BlockSpec(memory_space=pl.ANY)],
        out_specs=pl.BlockSpec((MOE_BLOCK, width), lambda i, idx: (i, 0)),
        scratch_shapes=[pltpu.SemaphoreType.DMA(())])
    return pl.pallas_call(
        _gather_body, grid_spec=grid_spec, out_shape=jax.ShapeDtypeStruct((n_rows, width), x_words.dtype),
        compiler_params=_cparams(("arbitrary",)), name="moe_gather")(src_row, x_words)


def _moe_block_state(be_ref, nu_ref):
    i = pl.program_id(1)
    nu = nu_ref[0]
    cur = jnp.minimum(i, nu - 1)
    changed = jnp.logical_or(i == 0, be_ref[cur] != be_ref[jnp.maximum(cur - 1, 0)])
    return i < nu, jnp.logical_and(i < nu, changed)


def _moe_up_body(be_ref, nu_ref, x_ref, wg_ref, wu_ref, h_ref, wg_bf, wu_bf):
    used, changed = _moe_block_state(be_ref, nu_ref)

    @pl.when(changed)
    def _():
        wg_bf[...] = wg_ref[0].astype(BF)
        wu_bf[...] = wu_ref[0].astype(BF)

    @pl.when(used)
    def _():
        x = x_ref[...]
        a = jnp.dot(x, wg_bf[...], preferred_element_type=F32)
        b = jnp.dot(x, wu_bf[...], preferred_element_type=F32)
        h_ref[...] = (a * jax.nn.sigmoid(a) * b).astype(h_ref.dtype)

    @pl.when(jnp.logical_not(used))
    def _():
        h_ref[...] = jnp.zeros(h_ref.shape, h_ref.dtype)


def _moe_down_body(be_ref, nu_ref, h_ref, wd_ref, y_ref, wd_bf):
    used, changed = _moe_block_state(be_ref, nu_ref)

    @pl.when(changed)
    def _():
        wd_bf[...] = wd_ref[0].astype(BF)

    @pl.when(used)
    def _():
        y_ref[...] = jnp.dot(h_ref[...], wd_bf[...], preferred_element_type=F32)

    @pl.when(jnp.logical_not(used))
    def _():
        y_ref[...] = jnp.zeros(y_ref.shape, y_ref.dtype)


def moe_experts(xs, block_expert, n_used, w_gate, w_up, w_down):
    n_rows, d = xs.shape
    nblk = n_rows // MOE_BLOCK
    tf = 256
    tn = 1024
    blk = lambda j, i, be, nu: (jnp.minimum(i, nu[0] - 1), 0)
    exp_col = lambda j, i, be, nu: (be[jnp.minimum(i, nu[0] - 1)], 0, j)
    out_map = lambda j, i, be, nu: (i, j)
    hid = pl.pallas_call(
        _moe_up_body,
        grid_spec=pltpu.PrefetchScalarGridSpec(
            num_scalar_prefetch=2, grid=(MOE_FF // tf, nblk),
            in_specs=[pl.BlockSpec((MOE_BLOCK, d), blk), pl.BlockSpec((1, d, tf), exp_col),
                      pl.BlockSpec((1, d, tf), exp_col)],
            out_specs=pl.BlockSpec((MOE_BLOCK, tf), out_map),
            scratch_shapes=[pltpu.VMEM((d, tf), BF), pltpu.VMEM((d, tf), BF)]),
        out_shape=jax.ShapeDtypeStruct((n_rows, MOE_FF), BF),
        compiler_params=_cparams(("arbitrary", "arbitrary")), name="moe_up")(
            block_expert, n_used, xs, w_gate, w_up)
    return pl.pallas_call(
        _moe_down_body,
        grid_spec=pltpu.PrefetchScalarGridSpec(
            num_scalar_prefetch=2, grid=(d // tn, nblk),
            in_specs=[pl.BlockSpec((MOE_BLOCK, MOE_FF), blk), pl.BlockSpec((1, MOE_FF, tn), exp_col)],
            out_specs=pl.BlockSpec((MOE_BLOCK, tn), out_map),
            scratch_shapes=[pltpu.VMEM((MOE_FF, tn), BF)]),
        out_shape=jax.ShapeDtypeStruct((n_rows, d), F32),
        compiler_params=_cparams(("arbitrary", "arbitrary")), name="moe_down")(
            block_expert, n_used, hid, w_down)


COMBINE_ROWS = 64


def _combine_body(dest_ref, y_hbm, wt_ref, x_ref, o_ref, buf, sem):
    base = pl.program_id(0) * COMBINE_ROWS

    def issue(r, carry):
        for kk in range(2):
            _row_copy(y_hbm, buf.at[kk], dest_ref[2 * (base + r) + kk], r, sem).start()
        return carry

    def wait(r, carry):
        for kk in range(2):
            _row_copy(y_hbm, buf.at[kk], 0, r, sem).wait()
        return carry

    lax.fori_loop(0, COMBINE_ROWS, issue, 0)
    lax.fori_loop(0, COMBINE_ROWS, wait, 0)
    wt = wt_ref[...]
    o_ref[...] = x_ref[...] + wt[:, 0:1] * buf[0] + wt[:, 1:2] * buf[1]


def moe_combine(x, yb, dest, wts):
    m, d = x.shape
    tm = COMBINE_ROWS
    grid_spec = pltpu.PrefetchScalarGridSpec(
        num_scalar_prefetch=1, grid=(m // tm,),
        in_specs=[pl.BlockSpec(memory_space=pl.ANY), pl.BlockSpec((tm, LANE), lambda i, dst: (i, 0)),
                  pl.BlockSpec((tm, d), lambda i, dst: (i, 0))],
        out_specs=pl.BlockSpec((tm, d), lambda i, dst: (i, 0)),
        scratch_shapes=[pltpu.VMEM((2, tm, d), F32), pltpu.SemaphoreType.DMA(())])
    return pl.pallas_call(
        _combine_body, grid_spec=grid_spec, out_shape=jax.ShapeDtypeStruct((m, d), F32),
        compiler_params=_cparams(("arbitrary",)), name="moe_combine")(dest, yb, wts, x)


def hier_moe(x, gain, w_route, b_route, w_gate, w_up, w_down):
    m, d = x.shape
    xn, e_out, wt_out = moe_router(x, gain, w_route, b_route)
    e_flat = e_out[:, :2].reshape(2 * m)
    onehot = (e_flat[:, None] == jnp.arange(MOE_EXPERTS, dtype=jnp.int32)[None, :]).astype(jnp.int32)
    csum = jnp.cumsum(onehot, axis=0)
    rank = jnp.sum(csum * onehot, axis=1) - 1
    counts = csum[-1]
    padded = (counts + MOE_BLOCK - 1) // MOE_BLOCK * MOE_BLOCK
    pend = jnp.cumsum(padded)
    pstart = pend - padded
    dest = (pstart[e_flat] + rank).astype(jnp.int32)
    n_rows = 2 * m + MOE_EXPERTS * MOE_BLOCK
    nblk = n_rows // MOE_BLOCK
    src_row = jnp.zeros((n_rows,), jnp.int32).at[dest].set(jnp.arange(2 * m, dtype=jnp.int32) // 2)
    block_expert = jnp.minimum(
        jnp.sum(jnp.arange(nblk, dtype=jnp.int32)[:, None] * MOE_BLOCK >= pend[None, :], axis=1),
        MOE_EXPERTS - 1).astype(jnp.int32)
    n_used = (pend[-1:] // MOE_BLOCK).astype(jnp.int32)

    x_words = lax.bitcast_convert_type(xn.reshape(m, d // 2, 2), jnp.uint32)
    xs_words = moe_gather(x_words, src_row)
    xs = lax.bitcast_convert_type(xs_words, BF).reshape(n_rows, d)
    yb = moe_experts(xs, block_expert, n_used, w_gate, w_up, w_down)
    return moe_combine(x, yb, dest, wt_out)


def _prep_w_in(w_in):
    cuts = (512, 512, 1024, 16, 1024, 1024, 1024, 1024, 1024, 8, 8, 1024, 3072, 3072, 3072, 16384)
    starts = [0]
    for c in cuts:
        starts.append(starts[-1] + c)
    seg = [w_in[:, starts[i]:starts[i + 1]] for i in range(len(cuts))]
    a_q, a_k, a_v, a_lr, a_r, b_q, b_k, b_v, b_z, b_beta, b_a, c_u, d_q, d_k, d_v, gates = seg
    k = w_in.shape[0]
    zeros = lambda n: jnp.zeros((k, n), w_in.dtype)
    out = jnp.concatenate([a_q, a_k, a_v, a_r, b_q, b_k, b_v, b_z, c_u, d_q, d_k, d_v, gates,
                           a_lr, zeros(LANE - GLA_RANK), b_beta, b_a, zeros(LANE - 2 * GDN_HEADS),
                           zeros(N_COLS - OFF_BBA - LANE)], axis=1)
    return out.astype(BF)


def _s5_block_diag(bbar, by_state):
    gpb = S5_GROUPS // S5_NBLK
    eye = jnp.eye(gpb, dtype=bbar.dtype)
    x = bbar.reshape(S5_NBLK, gpb, S5_STATE, S5_GROUP)
    if by_state:
        out = jnp.einsum("mgpi,gh->mgphi", x, eye)
        return out.reshape(S5_NBLK, gpb * S5_STATE, gpb * S5_GROUP)
    out = jnp.einsum("mgpi,gh->mgihp", x, eye)
    return out.reshape(S5_NBLK, gpb * S5_GROUP, gpb * S5_STATE)


def _pad_rows(x, t_valid, t_len):
    nb = x.shape[0] // t_valid
    x = x.reshape(nb, t_valid, x.shape[1])
    return jnp.pad(x, ((0, 0), (0, t_len - t_valid), (0, 0))).reshape(nb * t_len, x.shape[2])


def _layer(x, wl, n_p, bp, tp, bs, ts, states):
    m = x.shape[0]
    (gla_s0, gdn_s0, conv_s0, s5_s0, win_s0, mem_s0) = states
    xn = rmsnorm(x, wl["norm_mix"])
    p = matmul(xn, wl["w_in_r"], tm_cap=1032, tn=1024)

    ps = jnp.concatenate([_pad_rows(p[n_p:, :OFF_DQ], ts, CHUNK),
                          _pad_rows(p[n_p:, OFF_ALR:OFF_ALR + 2 * LANE], ts, CHUNK)], axis=1)
    s_alr, s_bba = OFF_DQ, OFF_DQ + LANE

    wa_pad = jnp.pad(wl["gla_w_a2"], ((0, LANE - GLA_RANK), (0, 0)))
    ba = wl["gla_b_a2"].reshape(1, -1)
    gn_a = wl["gla_norm"].reshape(1, -1)
    zeros_a = jnp.zeros((bp, GLA_HEADS, GLA_DV, GLA_DK), F32)
    oa_p, gla_p = gla_mixer(p, OFF_ALR, 0, bp, tp, tp, wa_pad, ba, gn_a, zeros_a)
    oa_s, gla_s = gla_mixer(ps, s_alr, 0, bs, CHUNK, ts, wa_pad, ba, gn_a, jnp.swapaxes(gla_s0, 2, 3))
    o_a = jnp.concatenate([oa_p, oa_s.reshape(bs, CHUNK, MIX_W)[:, :ts].reshape(bs * ts, MIX_W)], axis=0)

    alog_pad = jnp.zeros((1, LANE), F32).at[0, GDN_HEADS:2 * GDN_HEADS].set(wl["gdn_a_log"])
    dtb_pad = jnp.zeros((1, LANE), F32).at[0, GDN_HEADS:2 * GDN_HEADS].set(wl["gdn_dt_bias"])
    gn_b = wl["gdn_norm"].reshape(1, -1)
    pad_conv = lambda c: jnp.pad(c, ((0, 0), (SUBLANE - (GDN_CONV - 1), 0), (0, 0)))
    zeros_conv = jnp.zeros((bp, SUBLANE, 3 * MIX_W), F32)
    zeros_b = jnp.zeros((bp, GDN_HEADS, GDN_DK, GDN_DV), F32)
    ob_p, gdn_p = gdn_mixer(p, OFF_BBA, 0, bp, tp, tp, zeros_conv, wl["gdn_conv"], alog_pad, dtb_pad, gn_b, zeros_b)
    ob_s, gdn_s = gdn_mixer(ps, s_bba, 0, bs, CHUNK, ts, pad_conv(conv_s0), wl["gdn_conv"], alog_pad, dtb_pad, gn_b, gdn_s0)
    o_b = jnp.concatenate([ob_p, ob_s.reshape(bs, CHUNK, MIX_W)[:, :ts].reshape(bs * ts, MIX_W)], axis=0)
    qkv_p = p[:n_p, OFF_BQ:OFF_BZ].reshape(bp, tp, 3 * MIX_W)
    qkv_s = p[n_p:, OFF_BQ:OFF_BZ].reshape(bs, ts, 3 * MIX_W)
    conv_p = qkv_p[:, tp - (GDN_CONV - 1):]
    conv_s = jnp.concatenate([conv_s0, qkv_s], axis=1)[:, ts:]

    abar_r, abar_i, f_r, f_i = s5_discretise(wl["s5_a_re"], wl["s5_a_im"], wl["s5_log_dt"])
    bbar_r = f_r[..., None] * wl["s5_b_re"] - f_i[..., None] * wl["s5_b_im"]
    bbar_i = f_r[..., None] * wl["s5_b_im"] + f_i[..., None] * wl["s5_b_re"]
    wbr = _s5_block_diag(bbar_r, False).astype(BF)
    wbi = _s5_block_diag(bbar_i, False).astype(BF)
    wcr = _s5_block_diag(jnp.swapaxes(wl["s5_c_re"], 1, 2), True).astype(BF)
    wci = _s5_block_diag(jnp.swapaxes(wl["s5_c_im"], 1, 2), True).astype(BF)
    nst = S5_GROUPS * S5_STATE
    s5_args = (wbr, wbi, wcr, wci, abar_r.reshape(1, nst), abar_i.reshape(1, nst))
    s5_tail = (wl["s5_d"].reshape(1, -1), wl["s5_w_glu"].astype(BF), wl["s5_b_glu"].reshape(1, -1))
    zeros_c = jnp.zeros((bp, 1, nst), F32)
    oc_p, hr_p, hi_p = s5_mixer(p, 0, bp, tp, 256, *s5_args, zeros_c, zeros_c, *s5_tail)
    oc_s, hr_s, hi_s = s5_mixer(p, n_p, bs, ts, ts, *s5_args, s5_s0[..., 0].reshape(bs, 1, nst),
                                s5_s0[..., 1].reshape(bs, 1, nst), *s5_tail)
    o_c = jnp.concatenate([oc_p, oc_s], axis=0)
    s5_p = jnp.stack([hr_p.reshape(bp, S5_GROUPS, S5_STATE), hi_p.reshape(bp, S5_GROUPS, S5_STATE)], axis=-1)
    s5_s = jnp.stack([hr_s.reshape(bs, S5_GROUPS, S5_STATE), hi_s.reshape(bs, S5_GROUPS, S5_STATE)], axis=-1)

    qn = wl["dsa_q_norm"].reshape(1, -1)
    kn = wl["dsa_k_norm"].reshape(1, -1)
    outs_p, lses_p, outs_s, lses_s, win_p, win_s = [], [], [], [], [], []
    for gi, (window, dil) in enumerate(DSA_PATTERNS):
        o_g, l_g, k_g = dsa_prompt(p, bp, tp, gi, dil, qn, kn)
        outs_p.append(o_g)
        lses_p.append(l_g)
        v_g = p[:n_p, OFF_DV + gi * MIX_W:OFF_DV + (gi + 1) * MIX_W]
        kv = jnp.stack([k_g.reshape(bp, tp, DSA_HEADS, DSA_HD), v_g.reshape(bp, tp, DSA_HEADS, DSA_HD)], axis=2)
        win_p.append(kv[:, tp - min(window, tp):])
        cache = win_s0[gi]
        o_g, l_g, c_g = dsa_sample(p, n_p, bs, ts, gi, dil, cache.reshape(bs, cache.shape[1], 2 * MIX_W), qn, kn)
        outs_s.append(o_g)
        lses_s.append(l_g)
        win_s.append(c_g.reshape(cache.shape))
    o_d = jnp.concatenate([dsa_merge(outs_p, lses_p), dsa_merge(outs_s, lses_s)], axis=0)

    merged = branch_merge([o_a, o_b, o_c, o_d], wl["w_branch"].astype(BF), p)
    x = matmul(merged, wl["w_out"], residual=x, tm_cap=1032, tn=512)

    width = MEM_HEADS * MEM_HD
    kv_src = matmul(rmsnorm(wl["mem_prompt"], wl["norm_mem_src"]), wl["mem_w_kv"], tn=512)
    kv_p = kv_norm(kv_src, wl["mem_k_norm"].reshape(1, -1))
    q_all = matmul(rmsnorm(x, wl["norm_mem"]), wl["mem_w_q"], tn=width)
    mqn = wl["mem_q_norm"].reshape(1, -1)
    om_p = mem_attend(q_all, 0, bp, tp, 256, kv_p.reshape(bp, MEM_TOKENS, 2 * width), mqn)
    om_s = mem_attend(q_all, n_p, bs, ts, ts, mem_s0.reshape(bs, MEM_TOKENS, 2 * width), mqn)
    x = matmul(jnp.concatenate([om_p, om_s], axis=0), wl["mem_w_o"], residual=x, tm_cap=1032, tn=512)

    w_route = jnp.pad(jnp.concatenate([wl["moe_w_coarse"], wl["moe_w_fine"]], axis=1),
                      ((0, 0), (0, LANE - MOE_GROUPS - MOE_EXPERTS)))
    b_route = jnp.pad(jnp.concatenate([wl["moe_b_coarse"], wl["moe_b_fine"]]),
                      (0, LANE - MOE_GROUPS - MOE_EXPERTS)).reshape(1, LANE)
    x = hier_moe(x, wl["norm_moe"], w_route, b_route, wl["moe_w_gate"], wl["moe_w_up"], wl["moe_w_down"])

    mem_kv_p = kv_p.reshape(bp, MEM_TOKENS, 2, MEM_HEADS, MEM_HD)
    return x, (jnp.swapaxes(gla_p, 2, 3), jnp.swapaxes(gla_s, 2, 3), gdn_p, gdn_s, conv_p, conv_s,
               s5_p, s5_s, win_p, win_s, mem_kv_p)


def kernel(x_prompt, x_sample, state_gla, state_gdn, state_gdn_conv, state_s5, cache_win128, cache_win512, cache_win2048, cache_mem, mem_prompt, norm_mix, norm_mem, norm_mem_src, norm_moe, w_in, gla_w_a2, gla_b_a2, gla_norm, gdn_conv, gdn_a_log, gdn_dt_bias, gdn_norm, s5_a_re, s5_a_im, s5_log_dt, s5_b_re, s5_b_im, s5_c_re, s5_c_im, s5_d, s5_w_glu, s5_b_glu, dsa_q_norm, dsa_k_norm, w_branch, w_out, mem_w_q, mem_w_kv, mem_q_norm, mem_k_norm, mem_w_o, moe_w_coarse, moe_b_coarse, moe_w_fine, moe_b_fine, moe_w_gate, moe_w_up, moe_w_down):
    weights = dict(
        norm_mix=norm_mix, norm_mem=norm_mem, norm_mem_src=norm_mem_src, norm_moe=norm_moe,
        gla_w_a2=gla_w_a2, gla_b_a2=gla_b_a2, gla_norm=gla_norm,
        gdn_conv=gdn_conv, gdn_a_log=gdn_a_log, gdn_dt_bias=gdn_dt_bias, gdn_norm=gdn_norm,
        s5_a_re=s5_a_re, s5_a_im=s5_a_im, s5_log_dt=s5_log_dt, s5_b_re=s5_b_re, s5_b_im=s5_b_im,
        s5_c_re=s5_c_re, s5_c_im=s5_c_im, s5_d=s5_d, s5_w_glu=s5_w_glu, s5_b_glu=s5_b_glu,
        dsa_q_norm=dsa_q_norm, dsa_k_norm=dsa_k_norm, w_branch=w_branch, w_out=w_out,
        mem_w_q=mem_w_q, mem_w_kv=mem_w_kv, mem_q_norm=mem_q_norm, mem_k_norm=mem_k_norm, mem_w_o=mem_w_o,
        moe_w_coarse=moe_w_coarse, moe_b_coarse=moe_b_coarse, moe_w_fine=moe_w_fine, moe_b_fine=moe_b_fine,
        moe_w_gate=moe_w_gate, moe_w_up=moe_w_up, moe_w_down=moe_w_down)
    depth = w_in.shape[0]
    bp, tp, d = x_prompt.shape
    bs, ts, _ = x_sample.shape
    n_p = bp * tp
    assert tp % (CHUNK * 16) == 0 and ts <= CHUNK and (2 * (n_p + bs * ts)) % MOE_BLOCK == 0
    x = jnp.concatenate([x_prompt.reshape(n_p, d), x_sample.reshape(bs * ts, d)], axis=0)
    mem_rows = mem_prompt.reshape(bp * MEM_TOKENS, d)
    per_layer = []
    for l in range(depth):
        wl = {name: arr[l] for name, arr in weights.items()}
        wl["w_in_r"] = _prep_w_in(w_in[l])
        wl["mem_prompt"] = mem_rows
        states = (state_gla[l], state_gdn[l], state_gdn_conv[l], state_s5[l],
                  (cache_win128[l], cache_win512[l], cache_win2048[l]), cache_mem[l])
        x, outs = _layer(x, wl, n_p, bp, tp, bs, ts, states)
        per_layer.append(outs)
    stack = lambda i: jnp.stack([o[i] for o in per_layer])
    stack_win = lambda i, g: jnp.stack([o[i][g] for o in per_layer])
    return (x[:n_p].reshape(bp, tp, d), x[n_p:].reshape(bs, ts, d),
            stack(0), stack(1), stack(2), stack(3), stack(4), stack(5), stack(6), stack(7),
            stack_win(8, 0), stack_win(9, 0), stack_win(8, 1), stack_win(9, 1), stack_win(8, 2), stack_win(9, 2),
            stack(10))
```
